```python
import math
import jax, jax.numpy as jnp
from jax import lax
import numpy as np

D_MODEL = 2048
BATCH = 2
SEQ = 4096
DEPTH = 1
DEC_BATCH = 32
DEC_SEQ = 4
PAST_LEN = 8192
PAGE_SIZE = 128

MIX_W = D_MODEL
FOX_HEADS = 8
FOX_W = MIX_W // 2
FOX_HEAD_DIM = FOX_W // FOX_HEADS
LRU_W = MIX_W - FOX_W
LRU_BLOCKS = 8
LRU_BW = LRU_W // LRU_BLOCKS
CONV_W = 4
LRU_C = 8.0
IN_COLS = 3 * FOX_W + FOX_HEADS + 2 * LRU_W
Q_BLOCK = 128
N_MEM = 256
X_HEADS = 4
X_HEAD_DIM = 128
X_W = X_HEADS * X_HEAD_DIM
N_EXPERTS = 32
TOP_K = 4
EXPERT_FF = D_MODEL
SWIGLU_LIMIT = 7.0
SWIGLU_ALPHA = 1.702
MOE_BLOCK = 128
EPS = 1e-6

kernel_name = "hymba_fox_rglru_moe_step"


def rms_norm(x, g):
    xf = x.astype(jnp.float32)
    y = xf * lax.rsqrt(jnp.mean(xf * xf, axis=-1, keepdims=True) + EPS)
    return (y * g.astype(jnp.float32)).astype(x.dtype)


def split_mix_proj(h, w_in, b_forget):
    B, T, _ = h.shape
    p = jnp.einsum('btd,dc->btc', h, w_in)
    cuts = [FOX_W, 2 * FOX_W, 3 * FOX_W, 3 * FOX_W + FOX_HEADS, 3 * FOX_W + FOX_HEADS + LRU_W]
    q, k, v, f_logit, xb, gb = jnp.split(p, cuts, axis=-1)
    q = q.reshape(B, T, FOX_HEADS, FOX_HEAD_DIM)
    k = k.reshape(B, T, FOX_HEADS, FOX_HEAD_DIM)
    v = v.reshape(B, T, FOX_HEADS, FOX_HEAD_DIM)
    logf = jax.nn.log_sigmoid((f_logit + b_forget).astype(jnp.float32))
    return q, k, v, logf, xb, gb


def fox_attend(q, cq, pos_q, k, v, ck, pos_k):
    s = jnp.einsum('bqhd,bkhd->bhqk', q, k).astype(jnp.float32) * (FOX_HEAD_DIM ** -0.5)
    bias = jnp.transpose(cq, (0, 2, 1))[..., :, None] - jnp.transpose(ck, (0, 2, 1))[..., None, :]
    visible = pos_k[None, None, None, :] <= pos_q[None, None, :, None]
    s = jnp.where(visible, s + bias, -jnp.inf)
    p = jax.nn.softmax(s, axis=-1).astype(v.dtype)
    return jnp.einsum('bhqk,bkhd->bqhd', p, v)


def fox_prompt(q, k, v, logf):
    B, S = q.shape[:2]
    c = jnp.cumsum(logf, axis=1)
    pos = jnp.arange(S, dtype=jnp.int32)
    nb = S // Q_BLOCK
    qs = q.reshape(B, nb, Q_BLOCK, FOX_HEADS, FOX_HEAD_DIM).transpose(1, 0, 2, 3, 4)
    cs = c.reshape(B, nb, Q_BLOCK, FOX_HEADS).transpose(1, 0, 2, 3)
    ps = pos.reshape(nb, Q_BLOCK)

    def block(args):
        qb, cb, pb = args
        return fox_attend(qb, cb, pb, k, v, c, pos)

    o = lax.map(block, (qs, cs, ps))
    return o.transpose(1, 0, 2, 3, 4).reshape(B, S, FOX_W)


def fox_sample(q, k, v, logf, cache_k, cache_v, cache_logf, page_table):
    DB, T = q.shape[:2]
    past = page_table.shape[1] * PAGE_SIZE
    kp = cache_k[page_table].reshape(DB, past, FOX_HEADS, FOX_HEAD_DIM)
    vp = cache_v[page_table].reshape(DB, past, FOX_HEADS, FOX_HEAD_DIM)
    lp = cache_logf[page_table].reshape(DB, past, FOX_HEADS).astype(jnp.float32)
    k_all = jnp.concatenate([kp, k.astype(kp.dtype)], axis=1)
    v_all = jnp.concatenate([vp, v.astype(vp.dtype)], axis=1)
    c = jnp.cumsum(jnp.concatenate([lp, logf], axis=1), axis=1)
    pos = jnp.arange(past + T, dtype=jnp.int32)
    o = fox_attend(q, c[:, past:], pos[past:], k_all, v_all, c, pos)
    return o.reshape(DB, T, FOX_W).astype(q.dtype)


def causal_conv(u, buf, w, b):
    T = u.shape[1]
    ext = jnp.concatenate([buf.astype(u.dtype), u], axis=1)
    out = sum(ext[:, j:j + T] * w[j] for j in range(CONV_W)) + b
    return out, ext[:, -(CONV_W - 1):]


def rg_lru(xc, h0, w_r, b_r, w_i, b_i, lam):
    B, T, _ = xc.shape
    xb = xc.reshape(B, T, LRU_BLOCKS, LRU_BW)
    r_gate = jax.nn.sigmoid(jnp.einsum('btni,nij->btnj', xb, w_r).reshape(B, T, LRU_W) + b_r)
    i_gate = jax.nn.sigmoid(jnp.einsum('btni,nij->btnj', xb, w_i).reshape(B, T, LRU_W) + b_i)
    log_a = (-LRU_C * r_gate.astype(jnp.float32)) * jax.nn.softplus(-lam.astype(jnp.float32))
    a = jnp.exp(log_a)
    bterm = jnp.sqrt(-jnp.expm1(2.0 * log_a)) * (i_gate * xc).astype(jnp.float32)
    bterm = bterm.at[:, 0].add(a[:, 0] * h0.astype(jnp.float32))

    def combine(left, right):
        a1, b1 = left
        a2, b2 = right
        return a1 * a2, a2 * b1 + b2

    _, h = lax.associative_scan(combine, (a, bterm), axis=1)
    return h.astype(xc.dtype), h[:, -1].astype(xc.dtype)


def memory_kv(mem, g, w_k, w_v):
    m = rms_norm(mem, g)
    B, M, _ = m.shape
    mk = jnp.einsum('bmd,dc->bmc', m, w_k).reshape(B, M, X_HEADS, X_HEAD_DIM)
    mv = jnp.einsum('bmd,dc->bmc', m, w_v).reshape(B, M, X_HEADS, X_HEAD_DIM)
    return mk, mv


def cross_attend(h, mk, mv, w_q, w_o):
    B, T, _ = h.shape
    q = jnp.einsum('btd,dc->btc', h, w_q).reshape(B, T, X_HEADS, X_HEAD_DIM)
    s = jnp.einsum('bqhd,bkhd->bhqk', q, mk).astype(jnp.float32) * (X_HEAD_DIM ** -0.5)
    p = jax.nn.softmax(s, axis=-1).astype(mv.dtype)
    o = jnp.einsum('bhqk,bkhd->bqhd', p, mv).reshape(B, T, X_W)
    return jnp.einsum('btc,cd->btd', o, w_o).astype(h.dtype)


def clamped_swiglu_expert(xb, w_gu, b_gu, w_dn, b_dn):
    gu = xb @ w_gu + b_gu
    g, u = gu[:, :EXPERT_FF], gu[:, EXPERT_FF:]
    g = jnp.minimum(g, SWIGLU_LIMIT)
    u = jnp.clip(u, -SWIGLU_LIMIT, SWIGLU_LIMIT)
    act = g * jax.nn.sigmoid(SWIGLU_ALPHA * g)
    return ((u + 1.0) * act) @ w_dn + b_dn


def moe(h, w_router, b_router, w_gu, b_gu, w_dn, b_dn):
    shp = h.shape
    xt = h.reshape(-1, D_MODEL)
    n = xt.shape[0]
    logits = (xt @ w_router).astype(jnp.float32) + b_router.astype(jnp.float32)
    top_v, top_e = lax.top_k(logits, TOP_K)
    gates = jax.nn.softmax(top_v, axis=-1)
    flat_e = top_e.reshape(-1)
    order = jnp.argsort(flat_e)
    e_sorted = flat_e[order]
    tok_sorted = (order // TOP_K).astype(jnp.int32)
    gate_sorted = gates.reshape(-1)[order]
    counts = jnp.zeros((N_EXPERTS,), jnp.int32).at[flat_e].add(1)
    padded = (counts + MOE_BLOCK - 1) // MOE_BLOCK * MOE_BLOCK
    start = jnp.cumsum(counts) - counts
    pend = jnp.cumsum(padded)
    pstart = pend - padded
    dest = pstart[e_sorted] + jnp.arange(n * TOP_K, dtype=jnp.int32) - start[e_sorted]
    n_blocks = -(-(n * TOP_K) // MOE_BLOCK) + N_EXPERTS
    n_rows = n_blocks * MOE_BLOCK
    row_tok = jnp.full((n_rows,), n, jnp.int32).at[dest].set(tok_sorted)
    row_gate = jnp.zeros((n_rows,), jnp.float32).at[dest].set(gate_sorted)
    block_start = jnp.arange(n_blocks, dtype=jnp.int32) * MOE_BLOCK
    block_e = jnp.minimum(jnp.searchsorted(pend, block_start, side='right'), N_EXPERTS - 1)
    x_pad = jnp.concatenate([xt, jnp.zeros((1, D_MODEL), xt.dtype)], axis=0)

    def run_block(args):
        e, rt = args
        return clamped_swiglu_expert(x_pad[rt], w_gu[e], b_gu[e], w_dn[e], b_dn[e])

    y_rows = lax.map(run_block, (block_e, row_tok.reshape(n_blocks, MOE_BLOCK)))
    y_rows = y_rows.reshape(n_rows, D_MODEL) * row_gate[:, None].astype(y_rows.dtype)
    y = jnp.zeros((n + 1, D_MODEL), y_rows.dtype).at[row_tok].add(y_rows)[:n]
    return y.reshape(shp).astype(h.dtype)


def setup_inputs(seed: int = 0) -> dict:
    key = jax.random.key(seed)
    keys = jax.random.split(key, 48)
    counter = [0]

    def nxt():
        k = keys[counter[0]]
        counter[0] += 1
        return k

    def nrm(shape, scale=1.0):
        return jax.random.normal(nxt(), shape, jnp.float32) * scale

    def gain(shape):
        return 1.0 + nrm(shape, 0.02)

    n_pages = PAST_LEN // PAGE_SIZE
    n_used = DEC_BATCH * n_pages
    n_pool = n_used + n_used // 4
    page_table = jax.random.permutation(nxt(), n_pool)[:n_used].reshape(DEC_BATCH, n_pages).astype(jnp.int32)

    u = jax.random.uniform(nxt(), (DEPTH, LRU_W), jnp.float32, 0.9, 0.999)
    a_base = u ** (1.0 / LRU_C)
    lru_lambda = jnp.log(a_base) - jnp.log1p(-a_base)

    return {
        'x_prompt': nrm((BATCH, SEQ, D_MODEL)),
        'x_sample': nrm((DEC_BATCH, DEC_SEQ, D_MODEL)),
        'cache_fox_k': nrm((DEPTH, n_pool, PAGE_SIZE, FOX_HEADS, FOX_HEAD_DIM)),
        'cache_fox_v': nrm((DEPTH, n_pool, PAGE_SIZE, FOX_HEADS, FOX_HEAD_DIM)),
        'cache_fox_logf': jax.nn.log_sigmoid(nrm((DEPTH, n_pool, PAGE_SIZE, FOX_HEADS)) + 2.5),
        'state_conv': nrm((DEPTH, DEC_BATCH, CONV_W - 1, LRU_W)),
        'state_lru': nrm((DEPTH, DEC_BATCH, LRU_W), 0.5),
        'cache_mem_k': nrm((DEPTH, DEC_BATCH, N_MEM, X_HEADS, X_HEAD_DIM)),
        'cache_mem_v': nrm((DEPTH, DEC_BATCH, N_MEM, X_HEADS, X_HEAD_DIM)),
        'page_table': page_table,
        'mem_prompt': nrm((BATCH, N_MEM, D_MODEL)),
        'norm_mix': gain((DEPTH, D_MODEL)),
        'w_in': nrm((DEPTH, D_MODEL, IN_COLS), D_MODEL ** -0.5),
        'b_forget': jax.random.uniform(nxt(), (DEPTH, FOX_HEADS), jnp.float32, 1.0, 4.0),
        'conv_w': nrm((DEPTH, CONV_W, LRU_W), CONV_W ** -0.5),
        'conv_b': nrm((DEPTH, LRU_W), 0.01),
        'w_rgate': nrm((DEPTH, LRU_BLOCKS, LRU_BW, LRU_BW), LRU_BW ** -0.5),
        'b_rgate': nrm((DEPTH, LRU_W), 0.01),
        'w_igate': nrm((DEPTH, LRU_BLOCKS, LRU_BW, LRU_BW), LRU_BW ** -0.5),
        'b_igate': nrm((DEPTH, LRU_W), 0.01),
        'lru_lambda': lru_lambda,
        'fox_out_norm': gain((DEPTH, FOX_W)),
        'lru_out_norm': gain((DEPTH, LRU_W)),
        'w_out': nrm((DEPTH, MIX_W, D_MODEL), MIX_W ** -0.5),
        'norm_xattn': gain((DEPTH, D_MODEL)),
        'norm_mem': gain((DEPTH, D_MODEL)),
        'w_xq': nrm((DEPTH, D_MODEL, X_W), D_MODEL ** -0.5),
        'w_xk': nrm((DEPTH, D_MODEL, X_W), D_MODEL ** -0.5),
        'w_xv': nrm((DEPTH, D_MODEL, X_W), D_MODEL ** -0.5),
        'w_xo': nrm((DEPTH, X_W, D_MODEL), X_W ** -0.5),
        'norm_ffn': gain((DEPTH, D_MODEL)),
        'w_router': nrm((DEPTH, D_MODEL, N_EXPERTS), D_MODEL ** -0.5),
        'b_router': nrm((DEPTH, N_EXPERTS), 0.01),
        'w_gate_up': nrm((DEPTH, N_EXPERTS, D_MODEL, 2 * EXPERT_FF), D_MODEL ** -0.5),
        'b_gate_up': nrm((DEPTH, N_EXPERTS, 2 * EXPERT_FF), 0.01),
        'w_down': nrm((DEPTH, N_EXPERTS, EXPERT_FF, D_MODEL), EXPERT_FF ** -0.5),
        'b_down': nrm((DEPTH, N_EXPERTS, D_MODEL), 0.01),
        'norm_final': gain((D_MODEL,)),
    }


def reference(x_prompt, x_sample, cache_fox_k, cache_fox_v, cache_fox_logf, state_conv, state_lru,
              cache_mem_k, cache_mem_v, page_table, mem_prompt,
              norm_mix, w_in, b_forget, conv_w, conv_b, w_rgate, b_rgate, w_igate, b_igate,
              lru_lambda, fox_out_norm, lru_out_norm, w_out,
              norm_xattn, norm_mem, w_xq, w_xk, w_xv, w_xo,
              norm_ffn, w_router, b_router, w_gate_up, b_gate_up, w_down, b_down, norm_final):

    def layer(x, l, attend_fox, conv_buf, h0, mk, mv):
        h = rms_norm(x, norm_mix[l])
        q, k, v, logf, xb, gb = split_mix_proj(h, w_in[l], b_forget[l])
        fox = attend_fox(q, k, v, logf)
        xc, new_buf = causal_conv(xb, conv_buf, conv_w[l], conv_b[l])
        hs, h_last = rg_lru(xc, h0, w_rgate[l], b_rgate[l], w_igate[l], b_igate[l], lru_lambda[l])
        lru = hs * jax.nn.gelu(gb)
        mix = jnp.concatenate([rms_norm(fox, fox_out_norm[l]), rms_norm(lru, lru_out_norm[l])], axis=-1)
        x = x + jnp.einsum('btc,cd->btd', mix, w_out[l]).astype(x.dtype)
        x = x + cross_attend(rms_norm(x, norm_xattn[l]), mk, mv, w_xq[l], w_xo[l])
        x = x + moe(rms_norm(x, norm_ffn[l]), w_router[l], b_router[l],
                    w_gate_up[l], b_gate_up[l], w_down[l], b_down[l])
        return x, (k, v, logf, new_buf, h_last)

    xp, xs = x_prompt, x_sample
    bp = x_prompt.shape[0]
    st_prompt = []
    st_sample = []
    for l in range(DEPTH):
        mk_p, mv_p = memory_kv(mem_prompt, norm_mem[l], w_xk[l], w_xv[l])
        conv0 = jnp.zeros((bp, CONV_W - 1, LRU_W), x_prompt.dtype)
        h00 = jnp.zeros((bp, LRU_W), x_prompt.dtype)
        xp, stp = layer(xp, l, fox_prompt, conv0, h00, mk_p, mv_p)
        st_prompt.append(stp + (mk_p, mv_p))

        def fox_past(q, k, v, logf, l=l):
            return fox_sample(q, k, v, logf, cache_fox_k[l], cache_fox_v[l], cache_fox_logf[l], page_table)

        xs, sts = layer(xs, l, fox_past, state_conv[l], state_lru[l], cache_mem_k[l], cache_mem_v[l])
        st_sample.append(sts)

    def stack(items, j):
        return jnp.stack([it[j] for it in items], axis=0)

    y_prompt = rms_norm(xp, norm_final)
    y_sample = rms_norm(xs, norm_final)
    return (y_prompt, y_sample,
            stack(st_prompt, 0), stack(st_prompt, 1), stack(st_prompt, 2),
            stack(st_prompt, 3), stack(st_prompt, 4), stack(st_prompt, 5), stack(st_prompt, 6),
            stack(st_sample, 0), stack(st_sample, 1), stack(st_sample, 2),
            stack(st_sample, 3), stack(st_sample, 4))
```

```python
import functools

import jax
import jax.numpy as jnp
from jax import lax
from jax.experimental import pallas as pl
from jax.experimental.pallas import tpu as pltpu

F32, BF16, I32 = jnp.float32, jnp.bfloat16, jnp.int32
NEG_INF = float("-inf")

D_MODEL = 2048
FOX_HEADS = 8
HEAD_DIM = 128
FOX_W = FOX_HEADS * HEAD_DIM
LRU_W = 1024
LRU_BLOCKS = 8
LRU_BW = LRU_W // LRU_BLOCKS
CONV_W = 4
LRU_C = 8.0
PAGE_SIZE = 128
X_HEADS = 4
X_HEAD_DIM = 128
X_W = X_HEADS * X_HEAD_DIM
N_EXPERTS = 32
TOP_K = 4
EXPERT_FF = 2048
SWIGLU_LIMIT = 7.0
SWIGLU_ALPHA = 1.702
EPS = 1e-6

LANES = 128
VMEM_LIMIT_BYTES = 56 * 1024 * 1024

ARB, PAR = "arbitrary", "parallel"


def _params(*sem):
    return pltpu.CompilerParams(dimension_semantics=sem, vmem_limit_bytes=VMEM_LIMIT_BYTES)


def _rms(x, g):
    return x * lax.rsqrt(jnp.mean(x * x, axis=-1, keepdims=True) + EPS) * g


def _split3(x):
    hi = x.astype(BF16)
    r1 = x - hi.astype(F32)
    mid = r1.astype(BF16)
    lo = (r1 - mid.astype(F32)).astype(BF16)
    return hi, mid, lo


def _dot(a, b):
    return jnp.dot(a, b, preferred_element_type=F32)


def _dot_nt(a, b):
    return lax.dot_general(a, b, (((1,), (1,)), ((), ())), preferred_element_type=F32)


def _exact_dot_01(x, ones):
    hi, mid, lo = _split3(x)
    return (_dot(hi, ones) + _dot(mid, ones)) + _dot(lo, ones)


def _exact_01_dot(ones, x):
    hi, mid, lo = _split3(x)
    return (_dot(ones, hi) + _dot(ones, mid)) + _dot(ones, lo)


def _log_sigmoid(z):
    return jnp.minimum(z, 0.0) - jnp.log1p(jnp.exp(-jnp.abs(z)))


def _softplus(z):
    return jnp.maximum(z, 0.0) + jnp.log1p(jnp.exp(-jnp.abs(z)))


def _expm1(y):
    ey = jnp.exp(y)
    return jnp.where(jnp.abs(y) > 0.5, ey - 1.0, jnp.tanh(0.5 * y) * (ey + 1.0))


def _gelu_tanh(x):
    c = 0.7978845608028654
    return 0.5 * x * (1.0 + jnp.tanh(c * (x + 0.044715 * (x * x * x))))


def _in_proj_kernel(x_ref, g_ref, w_ref, wf_ref, bf_ref,
                    q_ref, k_ref, v_ref, xb_ref, gb_ref, lf_ref, h_ref):
    j = pl.program_id(1)

    @pl.when(j == 0)
    def _():
        h = _rms(x_ref[...], g_ref[...]).astype(BF16)
        h_ref[...] = h
        lf_ref[...] = _log_sigmoid(_dot(h, wf_ref[...]) + bf_ref[...])

    res = _dot(h_ref[...], w_ref[...])
    for idx, ref in enumerate((q_ref, k_ref, v_ref, xb_ref, gb_ref)):
        @pl.when(j == idx)
        def _(ref=ref):
            ref[...] = res


def in_proj(x2d, g, w5, wf, bfp, tm):
    m = x2d.shape[0]
    wide = pl.BlockSpec((tm, FOX_W), lambda i, j: (i, 0))
    return pl.pallas_call(
        _in_proj_kernel,
        grid=(m // tm, 5),
        in_specs=[
            pl.BlockSpec((tm, D_MODEL), lambda i, j: (i, 0)),
            pl.BlockSpec((1, D_MODEL), lambda i, j: (0, 0)),
            pl.BlockSpec((D_MODEL, FOX_W), lambda i, j: (0, j)),
            pl.BlockSpec((D_MODEL, LANES), lambda i, j: (0, 0)),
            pl.BlockSpec((1, LANES), lambda i, j: (0, 0)),
        ],
        out_specs=[wide] * 5 + [pl.BlockSpec((tm, LANES), lambda i, j: (i, 0))],
        out_shape=[jax.ShapeDtypeStruct((m, FOX_W), F32)] * 5 + [jax.ShapeDtypeStruct((m, LANES), F32)],
        scratch_shapes=[pltpu.VMEM((tm, D_MODEL), BF16)],
        compiler_params=_params(PAR, ARB),
        name="in_proj",
    )(x2d, g, w5, wf, bfp)


def _cumsum_lanes_kernel(x_ref, o_ref, carry_ref):
    @pl.when(pl.program_id(1) == 0)
    def _():
        carry_ref[...] = jnp.zeros_like(carry_ref)

    x = x_ref[0]
    tc = x.shape[1]
    r = lax.broadcasted_iota(I32, (tc, tc), 0)
    c = lax.broadcasted_iota(I32, (tc, tc), 1)
    tri = jnp.where(r <= c, 1.0, 0.0).astype(BF16)
    cs = _exact_dot_01(x, tri) + carry_ref[...]
    o_ref[0] = cs
    carry_ref[...] = cs[:, tc - 1:tc]


def cumsum_lanes(x, tc):
    b, h, t = x.shape
    return pl.pallas_call(
        _cumsum_lanes_kernel,
        grid=(b, t // tc),
        in_specs=[pl.BlockSpec((1, h, tc), lambda i, j: (i, 0, j))],
        out_specs=pl.BlockSpec((1, h, tc), lambda i, j: (i, 0, j)),
        out_shape=jax.ShapeDtypeStruct((b, h, t), F32),
        scratch_shapes=[pltpu.VMEM((h, 1), F32)],
        compiler_params=_params(PAR, ARB),
        name="cumsum_logf",
    )(x)


def _fox_prompt_kernel(qi_ref, ki_ref, q_ref, k_ref, v_ref, cq_ref, ck_ref, o_ref,
                       m_ref, l_ref, acc_ref, *, scale):
    p = pl.program_id(2)
    qi = qi_ref[p]
    ki = ki_ref[p]

    @pl.when(ki == 0)
    def _():
        m_ref[...] = jnp.full_like(m_ref, NEG_INF)
        l_ref[...] = jnp.zeros_like(l_ref)
        acc_ref[...] = jnp.zeros_like(acc_ref)

    s = _dot_nt(q_ref[...].astype(BF16), k_ref[...].astype(BF16)) * scale
    s = s + (cq_ref[0, 0] - ck_ref[0, 0])

    def update(s):
        m_prev = m_ref[...]
        m_new = jnp.maximum(m_prev, jnp.max(s, axis=-1, keepdims=True))
        alpha = jnp.exp(m_prev - m_new)
        pr = jnp.exp(s - m_new)
        l_ref[...] = alpha * l_ref[...] + jnp.sum(pr, axis=-1, keepdims=True)
        acc_ref[...] = alpha * acc_ref[...] + _dot(pr.astype(BF16), v_ref[...].astype(BF16))
        m_ref[...] = m_new

    @pl.when(ki < qi)
    def _():
        update(s)

    @pl.when(ki == qi)
    def _():
        row = lax.broadcasted_iota(I32, s.shape, 0)
        col = lax.broadcasted_iota(I32, s.shape, 1)
        update(jnp.where(col <= row, s, NEG_INF))
        o_ref[...] = acc_ref[...] / l_ref[...]


def fox_prompt(q, k, v, c_col, c_row, batch, tq):
    m = q.shape[0]
    t = m // batch
    nq = t // tq
    pairs = [(a, b) for a in range(nq) for b in range(a + 1)]
    qi_tab = jnp.array([a for a, _ in pairs], I32)
    ki_tab = jnp.array([b for _, b in pairs], I32)
    blk = (tq, HEAD_DIM)
    grid_spec = pltpu.PrefetchScalarGridSpec(
        num_scalar_prefetch=2,
        grid=(batch, FOX_HEADS, len(pairs)),
        in_specs=[
            pl.BlockSpec(blk, lambda b, h, p, qi, ki: (b * nq + qi[p], h)),
            pl.BlockSpec(blk, lambda b, h, p, qi, ki: (b * nq + ki[p], h)),
            pl.BlockSpec(blk, lambda b, h, p, qi, ki: (b * nq + ki[p], h)),
            pl.BlockSpec((1, 1, tq, 1), lambda b, h, p, qi, ki: (b, h, qi[p], 0)),
            pl.BlockSpec((1, 1, 1, tq), lambda b, h, p, qi, ki: (b, h, 0, ki[p])),
        ],
        out_specs=pl.BlockSpec(blk, lambda b, h, p, qi, ki: (b * nq + qi[p], h)),
        scratch_shapes=[pltpu.VMEM((tq, 1), F32), pltpu.VMEM((tq, 1), F32), pltpu.VMEM(blk, F32)],
    )
    return pl.pallas_call(
        functools.partial(_fox_prompt_kernel, scale=HEAD_DIM ** -0.5),
        grid_spec=grid_spec,
        out_shape=jax.ShapeDtypeStruct((m, FOX_W), F32),
        compiler_params=_params(PAR, PAR, ARB),
        name="fox_prompt",
    )(qi_tab, ki_tab, q, k, v, c_col, c_row)


def _lru_coeffs(xc, wr_ref, br, wi_ref, bi, lam):
    xcb = xc.astype(BF16)
    rs, gs = [], []
    for n in range(LRU_BLOCKS):
        blk = xcb[:, n * LRU_BW:(n + 1) * LRU_BW]
        rs.append(_dot(blk, wr_ref[n]))
        gs.append(_dot(blk, wi_ref[n]))
    r_gate = jax.nn.sigmoid(jnp.concatenate(rs, axis=1) + br)
    i_gate = jax.nn.sigmoid(jnp.concatenate(gs, axis=1) + bi)
    log_a = (-LRU_C * r_gate) * _softplus(-lam)
    a = jnp.exp(log_a)
    bterm = jnp.sqrt(-_expm1(2.0 * log_a)) * (i_gate * xc)
    return a, bterm


def _lru_prompt_kernel(xb_ref, gb_ref, conv0_ref, h0_ref, cw_ref, cb_ref, wr_ref, br_ref,
                       wi_ref, bi_ref, lam_ref, o_ref, hl_ref, ext_ref, hc_ref, a_ref, b_ref):
    tt = xb_ref.shape[0]
    pad = 8
    keep = CONV_W - 1

    @pl.when(pl.program_id(1) == 0)
    def _():
        ext_ref[pad - keep:pad, :] = conv0_ref[0]
        hc_ref[...] = h0_ref[0]

    u = xb_ref[...]
    ext_ref[pad:pad + tt, :] = u
    cw = cw_ref[...]
    xc = cw[CONV_W - 1:CONV_W] * u + cb_ref[...]
    for j in range(keep):
        xc = xc + cw[j:j + 1] * ext_ref[pad - keep + j:pad - keep + j + tt, :]
    ext_ref[pad - keep:pad, :] = u[tt - keep:tt]

    a, bterm = _lru_coeffs(xc, wr_ref, br_ref[...], wi_ref, bi_ref[...], lam_ref[...])
    a_ref[...] = a
    b_ref[...] = bterm

    def chunk(c, h):
        sl = pl.ds(pl.multiple_of(c * 8, 8), 8)
        a8 = a_ref[sl, :]
        b8 = b_ref[sl, :]
        rows = []
        for r in range(8):
            h = a8[r:r + 1] * h + b8[r:r + 1]
            rows.append(h)
        a_ref[sl, :] = jnp.concatenate(rows, axis=0)
        return h

    h = lax.fori_loop(0, tt // 8, chunk, hc_ref[...])
    hc_ref[...] = h
    hl_ref[0] = h
    o_ref[...] = a_ref[...] * _gelu_tanh(gb_ref[...])


def lru_prompt(xb, gb, conv0, h0, cw, cb, wr, br, wi, bi, lam, batch, tt):
    m = xb.shape[0]
    nt = m // batch // tt
    row = pl.BlockSpec((tt, LRU_W), lambda b, t: (b * nt + t, 0))
    vec = pl.BlockSpec((1, LRU_W), lambda b, t: (0, 0))
    wblk = pl.BlockSpec((LRU_BLOCKS, LRU_BW, LRU_BW), lambda b, t: (0, 0, 0))
    return pl.pallas_call(
        _lru_prompt_kernel,
        grid=(batch, nt),
        in_specs=[row, row,
                  pl.BlockSpec((1, CONV_W - 1, LRU_W), lambda b, t: (b, 0, 0)),
                  pl.BlockSpec((1, 1, LRU_W), lambda b, t: (b, 0, 0)),
                  pl.BlockSpec((CONV_W, LRU_W), lambda b, t: (0, 0)),
                  vec, wblk, vec, wblk, vec, vec],
        out_specs=[row, pl.BlockSpec((1, 1, LRU_W), lambda b, t: (b, 0, 0))],
        out_shape=[jax.ShapeDtypeStruct((m, LRU_W), F32), jax.ShapeDtypeStruct((batch, 1, LRU_W), F32)],
        scratch_shapes=[pltpu.VMEM((tt + 8, LRU_W), F32), pltpu.VMEM((1, LRU_W), F32),
                        pltpu.VMEM((tt, LRU_W), F32), pltpu.VMEM((tt, LRU_W), F32)],
        compiler_params=_params(PAR, ARB),
        name="lru_prompt",
    )(xb, gb, conv0, h0, cw, cb, wr, br, wi, bi, lam)


def _lru_sample_kernel(xb_ref, gb_ref, conv_ref, h0_ref, cw_ref, cb_ref, wr_ref, br_ref,
                       wi_ref, bi_ref, lam_ref, o_ref, hl_ref, *, nb, nt):
    u = xb_ref[...]
    ext = jnp.concatenate([conv_ref[...], u], axis=0)
    cw = cw_ref[...]
    xc = cb_ref[...] + cw[0:1] * ext[0:nt * nb]
    for j in range(1, CONV_W):
        xc = xc + cw[j:j + 1] * ext[j * nb:(j + nt) * nb]
    a, bterm = _lru_coeffs(xc, wr_ref, br_ref[...], wi_ref, bi_ref[...], lam_ref[...])
    h = h0_ref[...]
    rows = []
    for t in range(nt):
        h = a[t * nb:(t + 1) * nb] * h + bterm[t * nb:(t + 1) * nb]
        rows.append(h)
    hl_ref[...] = h
    o_ref[...] = jnp.concatenate(rows, axis=0) * _gelu_tanh(gb_ref[...])


def lru_sample(xb_tm, gb_tm, conv_tm, h0, cw, cb, wr, br, wi, bi, lam, nb, nt):
    args = (xb_tm, gb_tm, conv_tm, h0, cw, cb, wr, br, wi, bi, lam)

    def full(a):
        nd = a.ndim
        return pl.BlockSpec(a.shape, lambda i, nd=nd: (0,) * nd)

    return pl.pallas_call(
        functools.partial(_lru_sample_kernel, nb=nb, nt=nt),
        grid=(1,),
        in_specs=[full(a) for a in args],
        out_specs=[pl.BlockSpec((nt * nb, LRU_W), lambda i: (0, 0)), pl.BlockSpec((nb, LRU_W), lambda i: (0, 0))],
        out_shape=[jax.ShapeDtypeStruct((nt * nb, LRU_W), F32), jax.ShapeDtypeStruct((nb, LRU_W), F32)],
        compiler_params=_params(ARB),
        name="lru_sample",
    )(*args)


def _out_proj_kernel(fox_ref, lru_ref, gf_ref, gl_ref, w_ref, x_ref, o_ref):
    nf = _rms(fox_ref[...], gf_ref[...]).astype(BF16)
    nl = _rms(lru_ref[...], gl_ref[...]).astype(BF16)
    y = _dot(nf, w_ref[0:FOX_W, :]) + _dot(nl, w_ref[FOX_W:FOX_W + LRU_W, :])
    o_ref[...] = x_ref[...] + y


def out_proj(fox, lru, gf, gl, w, x2d, tm):
    m = x2d.shape[0]
    half = pl.BlockSpec((tm, FOX_W), lambda i: (i, 0))
    full = pl.BlockSpec((tm, D_MODEL), lambda i: (i, 0))
    vec = pl.BlockSpec((1, FOX_W), lambda i: (0, 0))
    return pl.pallas_call(
        _out_proj_kernel,
        grid=(m // tm,),
        in_specs=[half, half, vec, vec, pl.BlockSpec((FOX_W + LRU_W, D_MODEL), lambda i: (0, 0)), full],
        out_specs=full,
        out_shape=jax.ShapeDtypeStruct((m, D_MODEL), F32),
        compiler_params=_params(PAR),
        name="out_proj",
    )(fox, lru, gf, gl, w, x2d)


def _norm_matmul_kernel(x_ref, g_ref, w_ref, o_ref):
    o_ref[...] = _dot(_rms(x_ref[...], g_ref[...]).astype(BF16), w_ref[...])


def norm_matmul(x2d, g, w, tm):
    m, n = x2d.shape[0], w.shape[1]
    return pl.pallas_call(
        _norm_matmul_kernel,
        grid=(m // tm,),
        in_specs=[pl.BlockSpec((tm, D_MODEL), lambda i: (i, 0)),
                  pl.BlockSpec((1, D_MODEL), lambda i: (0, 0)),
                  pl.BlockSpec((D_MODEL, n), lambda i: (0, 0))],
        out_specs=pl.BlockSpec((tm, n), lambda i: (i, 0)),
        out_shape=jax.ShapeDtypeStruct((m, n), F32),
        compiler_params=_params(PAR),
        name="norm_matmul",
    )(x2d, g, w)


def _xattn_kernel(x_ref, g_ref, wq_ref, mk_ref, mv_ref, wo_ref, o_ref, *, scale):
    x = x_ref[0]
    h = _rms(x, g_ref[...]).astype(BF16)
    q = _dot(h, wq_ref[...])
    mk = mk_ref[0].astype(BF16)
    mv = mv_ref[0].astype(BF16)
    outs = []
    for hd in range(X_HEADS):
        sl = slice(hd * X_HEAD_DIM, (hd + 1) * X_HEAD_DIM)
        s = _dot_nt(q[:, sl].astype(BF16), mk[:, sl]) * scale
        pr = jnp.exp(s - jnp.max(s, axis=-1, keepdims=True))
        pr = pr / jnp.sum(pr, axis=-1, keepdims=True)
        outs.append(_dot(pr.astype(BF16), mv[:, sl]))
    o = jnp.concatenate(outs, axis=1).astype(BF16)
    o_ref[0] = x + _dot(o, wo_ref[...])


def xattn(x3d, g, wq, mk, mv, wo, tm):
    nb, t, _ = x3d.shape
    n_mem = mk.shape[1]
    row = pl.BlockSpec((1, tm, D_MODEL), lambda b, i: (b, i, 0))
    mem = pl.BlockSpec((1, n_mem, X_W), lambda b, i: (b, 0, 0))
    return pl.pallas_call(
        functools.partial(_xattn_kernel, scale=X_HEAD_DIM ** -0.5),
        grid=(nb, t // tm),
        in_specs=[row, pl.BlockSpec((1, D_MODEL), lambda b, i: (0, 0)),
                  pl.BlockSpec((D_MODEL, X_W), lambda b, i: (0, 0)), mem, mem,
                  pl.BlockSpec((X_W, D_MODEL), lambda b, i: (0, 0))],
        out_specs=row,
        out_shape=jax.ShapeDtypeStruct(x3d.shape, F32),
        compiler_params=_params(PAR, PAR),
        name="xattn",
    )(x3d, g, wq, mk, mv, wo)


def _router_kernel(x_ref, g_ref, w_ref, b_ref, oi_ref, og_ref, cnt_ref, carry_ref):
    @pl.when(pl.program_id(0) == 0)
    def _():
        carry_ref[...] = jnp.zeros_like(carry_ref)

    tm = x_ref.shape[0]
    h = _rms(x_ref[...], g_ref[...])
    h_hi, h_mid, _ = _split3(h)
    w = w_ref[...]
    w_hi, w_mid, _ = _split3(w)
    logits = (_dot(h_hi, w_hi) + (_dot(h_hi, w_mid) + _dot(h_mid, w_hi))) + b_ref[...]
    lane = lax.broadcasted_iota(I32, (tm, LANES), 1)
    lane_f = lane.astype(F32)
    logits = jnp.where(lane < N_EXPERTS, logits, NEG_INF)

    sel_e, sel_v = [], []
    onehot = jnp.zeros((tm, LANES), F32)
    work = logits
    for _ in range(TOP_K):
        v = jnp.max(work, axis=-1, keepdims=True)
        e = jnp.min(jnp.where(work == v, lane_f, float(LANES)), axis=-1, keepdims=True).astype(I32)
        hit = lane == e
        onehot = jnp.where(hit, 1.0, onehot)
        work = jnp.where(hit, NEG_INF, work)
        sel_e.append(e)
        sel_v.append(v)

    ex = [jnp.exp(v - sel_v[0]) for v in sel_v]
    den = (ex[0] + ex[1]) + (ex[2] + ex[3])

    r = lax.broadcasted_iota(I32, (tm, tm), 0)
    c = lax.broadcasted_iota(I32, (tm, tm), 1)
    before = jnp.where(c < r, 1.0, 0.0).astype(BF16)
    rank = _dot(before, onehot.astype(BF16)) + carry_ref[...]

    oi = jnp.zeros((tm, LANES), I32)
    og = jnp.zeros((tm, LANES), F32)
    for k in range(TOP_K):
        pos = jnp.sum(jnp.where(lane == sel_e[k], rank, 0.0), axis=-1, keepdims=True).astype(I32)
        oi = jnp.where(lane == k, sel_e[k], oi)
        oi = jnp.where(lane == TOP_K + k, pos, oi)
        og = jnp.where(lane == k, ex[k] / den, og)
    oi_ref[...] = oi
    og_ref[...] = og
    carry = carry_ref[...] + jnp.sum(onehot, axis=0, keepdims=True)
    carry_ref[...] = carry
    cnt_ref[...] = carry


def router(x2d, g, w_pad, b_pad, tm):
    m = x2d.shape[0]
    out = pl.BlockSpec((tm, LANES), lambda i: (i, 0))
    one = pl.BlockSpec((1, LANES), lambda i: (0, 0))
    return pl.pallas_call(
        _router_kernel,
        grid=(m // tm,),
        in_specs=[pl.BlockSpec((tm, D_MODEL), lambda i: (i, 0)),
                  pl.BlockSpec((1, D_MODEL), lambda i: (0, 0)),
                  pl.BlockSpec((D_MODEL, LANES), lambda i: (0, 0)), one],
        out_specs=[out, out, one],
        out_shape=[jax.ShapeDtypeStruct((m, LANES), I32), jax.ShapeDtypeStruct((m, LANES), F32),
                   jax.ShapeDtypeStruct((1, LANES), F32)],
        scratch_shapes=[pltpu.VMEM((1, LANES), F32)],
        compiler_params=_params(ARB),
        name="router",
    )(x2d, g, w_pad, b_pad)


def _row_copy(src_hbm, src_row, dst, dst_row, sem):
    return pltpu.make_async_copy(src_hbm.at[pl.ds(src_row, 1)], dst.at[pl.ds(dst_row, 1)], sem)


def _moe_gather_kernel(tok_ref, x_hbm, g_ref, o_ref, buf, sem):
    tm = buf.shape[0]

    def start(r, c):
        _row_copy(x_hbm, tok_ref[0, 0, r], buf, r, sem.at[0]).start()
        return c

    def wait(r, c):
        _row_copy(x_hbm, tok_ref[0, 0, r], buf, r, sem.at[0]).wait()
        return c

    lax.fori_loop(0, tm, start, 0)
    lax.fori_loop(0, tm, wait, 0)
    o_ref[...] = _rms(buf[...], g_ref[...]).astype(BF16)


def moe_gather(row_tok, x2d, g, tm):
    nblk = row_tok.shape[0]
    return pl.pallas_call(
        _moe_gather_kernel,
        grid=(nblk,),
        in_specs=[pl.BlockSpec((1, 1, tm), lambda i: (i, 0, 0), memory_space=pltpu.SMEM),
                  pl.BlockSpec(memory_space=pl.ANY),
                  pl.BlockSpec((1, D_MODEL), lambda i: (0, 0))],
        out_specs=pl.BlockSpec((tm, D_MODEL), lambda i: (i, 0)),
        out_shape=jax.ShapeDtypeStruct((nblk * tm, D_MODEL), BF16),
        scratch_shapes=[pltpu.VMEM((tm, D_MODEL), F32), pltpu.SemaphoreType.DMA((1,))],
        compiler_params=_params(ARB),
        name="moe_gather",
    )(row_tok, x2d, g)


def _moe_expert_kernel(ue_ref, ub_ref, un_ref, x_ref, wg_ref, wu_ref, bg_ref, bu_ref, wd_ref, bd_ref,
                       o_ref, wgb, wub, wdb, acc, *, nf):
    u, f, j = pl.program_id(0), pl.program_id(1), pl.program_id(2)
    nb = un_ref[u]

    @pl.when((j == 0) & (nb > 0))
    def _():
        wgb[...] = wg_ref[0].astype(BF16)
        wub[...] = wu_ref[0].astype(BF16)
        wdb[...] = wd_ref[0].astype(BF16)

    @pl.when(j < nb)
    def _():
        x = x_ref[...]
        g = jnp.minimum(_dot(x, wgb[...]) + bg_ref[0], SWIGLU_LIMIT)
        up = jnp.clip(_dot(x, wub[...]) + bu_ref[0], -SWIGLU_LIMIT, SWIGLU_LIMIT)
        act = g * jax.nn.sigmoid(SWIGLU_ALPHA * g)
        part = _dot(((up + 1.0) * act).astype(BF16), wdb[...])

        @pl.when(f == 0)
        def _():
            acc[j] = part

        @pl.when((f > 0) & (f < nf - 1))
        def _():
            acc[j] = acc[j] + part

        @pl.when(f == nf - 1)
        def _():
            o_ref[...] = (acc[j] + part) + bd_ref[0]


def moe_expert(ue, ub, un, xs, w_gu, b_gu, w_dn, b_dn, tm, tf, jmax):
    nu = ue.shape[0]
    nf = EXPERT_FF // tf
    assert nf >= 2

    def blk(j, ub, un, u):
        return ub[u] + jnp.minimum(j, jnp.maximum(un[u] - 1, 0))

    def ft(f, un, u):
        return jnp.where(un[u] > 0, f, nf - 1)

    grid_spec = pltpu.PrefetchScalarGridSpec(
        num_scalar_prefetch=3,
        grid=(nu, nf, jmax),
        in_specs=[
            pl.BlockSpec((tm, D_MODEL), lambda u, f, j, ue, ub, un: (blk(j, ub, un, u), 0)),
            pl.BlockSpec((1, D_MODEL, tf), lambda u, f, j, ue, ub, un: (ue[u], 0, ft(f, un, u))),
            pl.BlockSpec((1, D_MODEL, tf), lambda u, f, j, ue, ub, un: (ue[u], 0, nf + ft(f, un, u))),
            pl.BlockSpec((1, 1, tf), lambda u, f, j, ue, ub, un: (ue[u], 0, ft(f, un, u))),
            pl.BlockSpec((1, 1, tf), lambda u, f, j, ue, ub, un: (ue[u], 0, nf + ft(f, un, u))),
            pl.BlockSpec((1, tf, D_MODEL), lambda u, f, j, ue, ub, un: (ue[u], ft(f, un, u), 0)),
            pl.BlockSpec((1, 1, D_MODEL), lambda u, f, j, ue, ub, un: (ue[u], 0, 0)),
        ],
        out_specs=pl.BlockSpec(
            (tm, D_MODEL),
            lambda u, f, j, ue, ub, un: (jnp.where(f == nf - 1, blk(j, ub, un, u), ub[u]), 0)),
        scratch_shapes=[pltpu.VMEM((D_MODEL, tf), BF16), pltpu.VMEM((D_MODEL, tf), BF16),
                        pltpu.VMEM((tf, D_MODEL), BF16), pltpu.VMEM((jmax, tm, D_MODEL), F32)],
    )
    return pl.pallas_call(
        functools.partial(_moe_expert_kernel, nf=nf),
        grid_spec=grid_spec,
        out_shape=jax.ShapeDtypeStruct((xs.shape[0], D_MODEL), F32),
        compiler_params=_params(ARB, ARB, ARB),
        name="moe_expert",
    )(ue, ub, un, xs, w_gu, w_gu, b_gu, b_gu, w_dn, b_dn)


def _moe_combine_kernel(dest_ref, x_ref, gates_ref, gfin_ref, y_hbm, o_ref, buf, sem, *, final_norm):
    tt = x_ref.shape[0]

    def start(t, c):
        for k in range(TOP_K):
            _row_copy(y_hbm, dest_ref[0, 0, t * TOP_K + k], buf.at[k], t, sem.at[0]).start()
        return c

    def wait(t, c):
        for k in range(TOP_K):
            _row_copy(y_hbm, dest_ref[0, 0, t * TOP_K + k], buf.at[k], t, sem.at[0]).wait()
        return c

    lax.fori_loop(0, tt, start, 0)
    lax.fori_loop(0, tt, wait, 0)
    gates = gates_ref[...]
    y = gates[:, 0:1] * buf[0]
    for k in range(1, TOP_K):
        y = y + gates[:, k:k + 1] * buf[k]
    out = x_ref[...] + y
    o_ref[...] = _rms(out, gfin_ref[...]) if final_norm else out


def moe_combine(dest, x2d, gates, gfin, y_rows, tt, final_norm):
    m = x2d.shape[0]
    row = pl.BlockSpec((tt, D_MODEL), lambda i: (i, 0))
    return pl.pallas_call(
        functools.partial(_moe_combine_kernel, final_norm=final_norm),
        grid=(m // tt,),
        in_specs=[pl.BlockSpec((1, 1, tt * TOP_K), lambda i: (i, 0, 0), memory_space=pltpu.SMEM),
                  row, pl.BlockSpec((tt, TOP_K), lambda i: (i, 0)),
                  pl.BlockSpec((1, D_MODEL), lambda i: (0, 0)),
                  pl.BlockSpec(memory_space=pl.ANY)],
        out_specs=row,
        out_shape=jax.ShapeDtypeStruct((m, D_MODEL), F32),
        scratch_shapes=[pltpu.VMEM((TOP_K, tt, D_MODEL), F32), pltpu.SemaphoreType.DMA((1,))],
        compiler_params=_params(ARB),
        name="moe_combine",
    )(dest, x2d, gates, gfin, y_rows)


def _moe_plan(e, pos, counts, tm, jmax, nblk_max, nu):
    nblk = (counts + tm - 1) // tm
    bend = jnp.cumsum(nblk)
    bstart = bend - nblk
    dest = bstart[e] * tm + pos
    npair = e.size
    row_tok = jnp.zeros((nblk_max * tm,), I32).at[dest.reshape(-1)].set(
        jnp.arange(npair, dtype=I32) // TOP_K)
    nun = (nblk + jmax - 1) // jmax
    uend = jnp.cumsum(nun)
    ustart = uend - nun
    uid = jnp.arange(nu, dtype=I32)
    valid = uid < uend[-1]
    ue = jnp.minimum(jnp.searchsorted(uend, jnp.where(valid, uid, uend[-1] - 1), side="right"),
                     N_EXPERTS - 1).astype(I32)
    k = uid - ustart[ue]
    ub = jnp.where(valid, bstart[ue] + k * jmax, bend[-1] - 1).astype(I32)
    un = jnp.where(valid, jnp.minimum(jmax, nblk[ue] - k * jmax), 0).astype(I32)
    return dest.astype(I32), row_tok.reshape(nblk_max, 1, tm), ue, ub, un


def moe_final(x2d, g_ffn, w_router_pad, b_router_pad, w_gu, b_gu, w_dn, b_dn, g_final, final_norm,
              tm_route, tm, tf, jmax, tt):
    n = x2d.shape[0]
    oi, og, cnt = router(x2d, g_ffn, w_router_pad, b_router_pad, tm_route)
    e = oi[:, :TOP_K]
    pos = oi[:, TOP_K:2 * TOP_K]
    gates = og[:, :TOP_K]
    counts = cnt[0, :N_EXPERTS].astype(I32)
    nblk_max = (n * TOP_K) // tm + N_EXPERTS
    nu = nblk_max // jmax + (N_EXPERTS * (jmax - 1)) // jmax + 1
    dest, row_tok, ue, ub, un = _moe_plan(e, pos, counts, tm, jmax, nblk_max, nu)
    xs = moe_gather(row_tok, x2d, g_ffn, tm)
    y_rows = moe_expert(ue, ub, un, xs, w_gu, b_gu, w_dn, b_dn, tm, tf, jmax)
    return moe_combine(dest.reshape(n // tt, 1, tt * TOP_K), x2d, gates, g_final, y_rows, tt, final_norm)


def _paged_cumsum_kernel(pt_ref, *refs, npg):
    page_refs = refs[:npg]
    lnew_ref = refs[npg]
    o_ref, cnew_ref, carry_ref = refs[npg + 1:]
    g = pl.program_id(1)

    @pl.when(g == 0)
    def _():
        carry_ref[...] = jnp.zeros_like(carry_ref)

    r = lax.broadcasted_iota(I32, (PAGE_SIZE, PAGE_SIZE), 0)
    c = lax.broadcasted_iota(I32, (PAGE_SIZE, PAGE_SIZE), 1)
    tri = jnp.where(c <= r, 1.0, 0.0).astype(BF16)
    carry = carry_ref[...]
    for i in range(npg):
        cs = _exact_01_dot(tri, page_refs[i][0]) + carry
        o_ref[0, i] = cs
        carry = cs[PAGE_SIZE - 1:PAGE_SIZE]
    carry_ref[...] = carry

    @pl.when(g == pl.num_programs(1) - 1)
    def _():
        ln = lnew_ref[0]
        rows = []
        acc = carry
        for t in range(ln.shape[0]):
            acc = acc + ln[t:t + 1]
            rows.append(acc)
        cnew_ref[0] = jnp.concatenate(rows, axis=0)


def paged_cumsum(page_table, cache_logf, logf_new, npg):
    nb, npages = page_table.shape
    nt = logf_new.shape[1]
    page_specs = [
        pl.BlockSpec((1, PAGE_SIZE, FOX_HEADS), lambda b, g, pt, i=i: (pt[b, g * npg + i], 0, 0))
        for i in range(npg)]
    grid_spec = pltpu.PrefetchScalarGridSpec(
        num_scalar_prefetch=1,
        grid=(nb, npages // npg),
        in_specs=page_specs + [pl.BlockSpec((1, nt, FOX_HEADS), lambda b, g, pt: (b, 0, 0))],
        out_specs=[pl.BlockSpec((1, npg, PAGE_SIZE, FOX_HEADS), lambda b, g, pt: (b, g, 0, 0)),
                   pl.BlockSpec((1, nt, FOX_HEADS), lambda b, g, pt: (b, 0, 0))],
        scratch_shapes=[pltpu.VMEM((1, FOX_HEADS), F32)],
    )
    return pl.pallas_call(
        functools.partial(_paged_cumsum_kernel, npg=npg),
        grid_spec=grid_spec,
        out_shape=[jax.ShapeDtypeStruct((nb, npages, PAGE_SIZE, FOX_HEADS), F32),
                   jax.ShapeDtypeStruct((nb, nt, FOX_HEADS), F32)],
        compiler_params=_params(PAR, ARB),
        name="paged_cumsum",
    )(page_table, *([cache_logf] * npg), logf_new)


def _fox_decode_kernel(pt_ref, *refs, npg, nt, scale):
    q_ref, cq_ref, kn_ref, vn_ref, ckn_ref, ck_ref = refs[:6]
    k_refs = refs[6:6 + npg]
    v_refs = refs[6 + npg:6 + 2 * npg]
    o_ref, m_ref, l_ref, acc_ref = refs[6 + 2 * npg:]
    g = pl.program_id(1)
    nq = nt * FOX_HEADS
    q = q_ref[0].astype(BF16)
    cq = cq_ref[0]

    def update(s, v):
        m_prev = m_ref[...]
        m_new = jnp.maximum(m_prev, jnp.max(s, axis=-1, keepdims=True))
        alpha = jnp.exp(m_prev - m_new)
        pr = jnp.exp(s - m_new)
        l_ref[...] = alpha * l_ref[...] + jnp.sum(pr, axis=-1, keepdims=True)
        acc_ref[...] = alpha * acc_ref[...] + _dot(pr.astype(BF16), v)
        m_ref[...] = m_new

    @pl.when(g == 0)
    def _():
        m_ref[...] = jnp.full_like(m_ref, NEG_INF)
        l_ref[...] = jnp.zeros_like(l_ref)
        acc_ref[...] = jnp.zeros_like(acc_ref)
        s = _dot_nt(q, kn_ref[0].astype(BF16)) * scale + (cq - ckn_ref[0])
        r = lax.broadcasted_iota(I32, (nq, nq), 0)
        c = lax.broadcasted_iota(I32, (nq, nq), 1)
        same_head = (r % FOX_HEADS) == (c % FOX_HEADS)
        causal = (c // FOX_HEADS) <= (r // FOX_HEADS)
        update(jnp.where(same_head & causal, s, NEG_INF), vn_ref[0].astype(BF16))

    nk = PAGE_SIZE * FOX_HEADS
    r = lax.broadcasted_iota(I32, (nq, nk), 0)
    c = lax.broadcasted_iota(I32, (nq, nk), 1)
    same_head = (r % FOX_HEADS) == (c % FOX_HEADS)
    for i in range(npg):
        s = _dot_nt(q, k_refs[i][0].astype(BF16)) * scale + (cq - ck_ref[0, i])
        update(jnp.where(same_head, s, NEG_INF), v_refs[i][0].astype(BF16))

    @pl.when(g == pl.num_programs(1) - 1)
    def _():
        o_ref[0] = acc_ref[...] / l_ref[...]


def fox_decode(page_table, q, cq, k_new, v_new, ck_new, ck_past, cache_k, cache_v, npg, nt):
    nb, npages = page_table.shape
    nq = nt * FOX_HEADS
    nk = PAGE_SIZE * FOX_HEADS
    row = pl.BlockSpec((1, nq, HEAD_DIM), lambda b, g, pt: (b, 0, 0))
    page_specs = [
        pl.BlockSpec((1, nk, HEAD_DIM), lambda b, g, pt, i=i: (pt[b, g * npg + i], 0, 0))
        for i in range(npg)]
    grid_spec = pltpu.PrefetchScalarGridSpec(
        num_scalar_prefetch=1,
        grid=(nb, npages // npg),
        in_specs=[row, pl.BlockSpec((1, nq, 1), lambda b, g, pt: (b, 0, 0)), row, row,
                  pl.BlockSpec((1, 1, nq), lambda b, g, pt: (b, 0, 0)),
                  pl.BlockSpec((1, npg, 1, nk), lambda b, g, pt: (b, g, 0, 0))] + page_specs + page_specs,
        out_specs=row,
        scratch_shapes=[pltpu.VMEM((nq, 1), F32), pltpu.VMEM((nq, 1), F32), pltpu.VMEM((nq, HEAD_DIM), F32)],
    )
    return pl.pallas_call(
        functools.partial(_fox_decode_kernel, npg=npg, nt=nt, scale=HEAD_DIM ** -0.5),
        grid_spec=grid_spec,
        out_shape=jax.ShapeDtypeStruct((nb, nq, HEAD_DIM), F32),
        compiler_params=_params(PAR, ARB),
        name="fox_decode",
    )(page_table, q, cq, k_new, v_new, ck_new, ck_past, *([cache_k] * npg), *([cache_v] * npg))


def kernel(x_prompt, x_sample, cache_fox_k, cache_fox_v, cache_fox_logf, state_conv, state_lru,
           cache_mem_k, cache_mem_v, page_table, mem_prompt,
           norm_mix, w_in, b_forget, conv_w, conv_b, w_rgate, b_rgate, w_igate, b_igate,
           lru_lambda, fox_out_norm, lru_out_norm, w_out,
           norm_xattn, norm_mem, w_xq, w_xk, w_xv, w_xo,
           norm_ffn, w_router, b_router, w_gate_up, b_gate_up, w_down, b_down, norm_final):
    depth = w_in.shape[0]
    bp, seq, _ = x_prompt.shape
    db, dt, _ = x_sample.shape
    n_mem = mem_prompt.shape[1]
    qkv_w = 3 * FOX_W
    g_final = norm_final.reshape(1, D_MODEL)

    xp = x_prompt.reshape(bp * seq, D_MODEL)
    xs = x_sample.reshape(db * dt, D_MODEL)
    st_p = [[] for _ in range(7)]
    st_s = [[] for _ in range(5)]

    for l in range(depth):
        last = l == depth - 1
        w_l = w_in[l]
        w5 = jnp.concatenate([w_l[:, :qkv_w], w_l[:, qkv_w + FOX_HEADS:]], axis=1).astype(BF16)
        wf = jnp.pad(w_l[:, qkv_w:qkv_w + FOX_HEADS], ((0, 0), (0, LANES - FOX_HEADS))).astype(BF16)
        bfp = jnp.pad(b_forget[l], (0, LANES - FOX_HEADS)).reshape(1, LANES)
        g_mix = norm_mix[l].reshape(1, D_MODEL)
        cw, cb = conv_w[l], conv_b[l].reshape(1, LRU_W)
        wr, wi = w_rgate[l].astype(BF16), w_igate[l].astype(BF16)
        br, bi = b_rgate[l].reshape(1, LRU_W), b_igate[l].reshape(1, LRU_W)
        lam = lru_lambda[l].reshape(1, LRU_W)
        gf, gl = fox_out_norm[l].reshape(1, FOX_W), lru_out_norm[l].reshape(1, LRU_W)
        w_o = w_out[l].astype(BF16)
        g_x = norm_xattn[l].reshape(1, D_MODEL)
        wq, wo = w_xq[l].astype(BF16), w_xo[l].astype(BF16)
        g_ffn = norm_ffn[l].reshape(1, D_MODEL)
        wr_pad = jnp.pad(w_router[l], ((0, 0), (0, LANES - N_EXPERTS)))
        br_pad = jnp.pad(b_router[l], (0, LANES - N_EXPERTS)).reshape(1, LANES)
        w_gu, w_dn = w_gate_up[l], w_down[l]
        b_gu = b_gate_up[l].reshape(N_EXPERTS, 1, 2 * EXPERT_FF)
        b_dn = b_down[l].reshape(N_EXPERTS, 1, D_MODEL)

        mem2d = mem_prompt.reshape(bp * n_mem, D_MODEL)
        g_mem = norm_mem[l].reshape(1, D_MODEL)
        mk_p = norm_matmul(mem2d, g_mem, w_xk[l].astype(BF16), 256)
        mv_p = norm_matmul(mem2d, g_mem, w_xv[l].astype(BF16), 256)

        q, k, v, xb, gb, lf = in_proj(xp, g_mix, w5, wf, bfp, 512)
        logf = lf[:, :FOX_HEADS]
        logf_t = logf.reshape(bp, seq, FOX_HEADS).transpose(0, 2, 1)
        c = cumsum_lanes(logf_t, 512)
        fox = fox_prompt(q, k, v, c.reshape(bp, FOX_HEADS, seq, 1), c.reshape(bp, FOX_HEADS, 1, seq), bp, 512)
        lru, h_last = lru_prompt(xb, gb, jnp.zeros((bp, CONV_W - 1, LRU_W), F32), jnp.zeros((bp, 1, LRU_W), F32),
                                 cw, cb, wr, br, wi, bi, lam, bp, 512)
        x1 = out_proj(fox, lru, gf, gl, w_o, xp, 256)
        x2 = xattn(x1.reshape(bp, seq, D_MODEL), g_x, wq, mk_p.reshape(bp, n_mem, X_W),
                   mv_p.reshape(bp, n_mem, X_W), wo, 512).reshape(bp * seq, D_MODEL)
        xp = moe_final(x2, g_ffn, wr_pad, br_pad, w_gu, b_gu, w_dn, b_dn, g_final, last,
                       tm_route=256, tm=256, tf=512, jmax=6, tt=128)

        st_p[0].append(k.reshape(bp, seq, FOX_HEADS, HEAD_DIM))
        st_p[1].append(v.reshape(bp, seq, FOX_HEADS, HEAD_DIM))
        st_p[2].append(logf.reshape(bp, seq, FOX_HEADS))
        st_p[3].append(xb.reshape(bp, seq, LRU_W)[:, seq - (CONV_W - 1):])
        st_p[4].append(h_last.reshape(bp, LRU_W))
        st_p[5].append(mk_p.reshape(bp, n_mem, X_HEADS, X_HEAD_DIM))
        st_p[6].append(mv_p.reshape(bp, n_mem, X_HEADS, X_HEAD_DIM))

        q, k, v, xb, gb, lf = in_proj(xs, g_mix, w5, wf, bfp, db * dt)
        logf = lf[:, :FOX_HEADS]
        npool = cache_fox_k.shape[1]
        c_past, c_new = paged_cumsum(page_table, cache_fox_logf[l], logf.reshape(db, dt, FOX_HEADS), 8)
        npages = page_table.shape[1]
        nq = dt * FOX_HEADS
        fox = fox_decode(
            page_table, q.reshape(db, nq, HEAD_DIM), c_new.reshape(db, nq, 1),
            k.reshape(db, nq, HEAD_DIM), v.reshape(db, nq, HEAD_DIM), c_new.reshape(db, 1, nq),
            c_past.reshape(db, npages, 1, PAGE_SIZE * FOX_HEADS),
            cache_fox_k[l].reshape(npool, PAGE_SIZE * FOX_HEADS, HEAD_DIM),
            cache_fox_v[l].reshape(npool, PAGE_SIZE * FOX_HEADS, HEAD_DIM), 4, dt,
        ).reshape(db * dt, FOX_W)

        def to_tm(a, rows):
            return a.reshape(db, rows, -1).transpose(1, 0, 2).reshape(rows * db, -1)

        lru_tm, h_last = lru_sample(to_tm(xb, dt), to_tm(gb, dt), to_tm(state_conv[l], CONV_W - 1),
                                    state_lru[l], cw, cb, wr, br, wi, bi, lam, db, dt)
        lru = lru_tm.reshape(dt, db, LRU_W).transpose(1, 0, 2).reshape(db * dt, LRU_W)
        x1 = out_proj(fox, lru, gf, gl, w_o, xs, db * dt)
        pad_t = 16
        x1p = jnp.pad(x1.reshape(db, dt, D_MODEL), ((0, 0), (0, pad_t - dt), (0, 0)))
        x2 = xattn(x1p, g_x, wq, cache_mem_k[l].reshape(db, n_mem, X_W),
                   cache_mem_v[l].reshape(db, n_mem, X_W), wo, pad_t)[:, :dt].reshape(db * dt, D_MODEL)
        xs = moe_final(x2, g_ffn, wr_pad, br_pad, w_gu, b_gu, w_dn, b_dn, g_final, last,
                       tm_route=db * dt, tm=128, tf=512, jmax=1, tt=128)

        st_s[0].append(k.reshape(db, dt, FOX_HEADS, HEAD_DIM))
        st_s[1].append(v.reshape(db, dt, FOX_HEADS, HEAD_DIM))
        st_s[2].append(logf.reshape(db, dt, FOX_HEADS))
        st_s[3].append(xb.reshape(db, dt, LRU_W)[:, dt - (CONV_W - 1):])
        st_s[4].append(h_last)

    y_prompt = xp.reshape(bp, seq, D_MODEL)
    y_sample = xs.reshape(db, dt, D_MODEL)
    return (y_prompt, y_sample, *[jnp.stack(s, axis=0) for s in st_p], *[jnp.stack(s, axis=0) for s in st_s])
```
